```python
import jax, jax.numpy as jnp
from jax import lax
import numpy as np

D_MODEL = 1024
BATCH = 8
SEQ = 4096
DEPTH = 4

CHUNK = 64
N_MIXERS = 2
N_A_LAYERS = (DEPTH + 1) // 2
N_B_LAYERS = DEPTH // 2

SGU_CHUNK = 2 * CHUNK
SGU_HALF = D_MODEL
SGU_GROUPS = 16
SGU_GROUP_DIM = SGU_HALF // SGU_GROUPS

RWKV_HEAD_DIM = 64
RWKV_HEADS = D_MODEL // RWKV_HEAD_DIM
DECAY_LORA = 64
AAA_LORA = 64
GATE_LORA = 160
GN_EPS = 64e-5
N_SHIFT_MIX = 6

FFN_DIM = 2816
CONV_WIDTH = 3
RMS_EPS = 1e-6

kernel_name = "hybrid_sgu_rwkv7_convffn_trunk"


def rms_norm(x, g):
    xf = x.astype(jnp.float32)
    y = xf * lax.rsqrt(jnp.mean(xf * xf, axis=-1, keepdims=True) + RMS_EPS)
    return (y * g.astype(jnp.float32)).astype(x.dtype)


def sgu_mixer(h, w_in, b_in, g_v, w_s, b_s, w_out):
    B, S, _ = h.shape
    z = jax.nn.gelu(h @ w_in + b_in, approximate=False)
    u, v = jnp.split(z, 2, axis=-1)
    v = rms_norm(v, g_v)
    n_blk = S // SGU_CHUNK
    v = v.reshape(B, n_blk, SGU_CHUNK, SGU_GROUPS, SGU_GROUP_DIM)
    causal = jnp.tril(jnp.ones((SGU_CHUNK, SGU_CHUNK), dtype=bool))
    w_causal = jnp.where(causal[None], w_s, 0)
    v = jnp.einsum('gts,bnsgc->bntgc', w_causal, v) + b_s.T[None, None, :, :, None]
    y = u * v.reshape(B, S, SGU_HALF)
    return y @ w_out


def token_shift(x):
    return jnp.pad(x, ((0, 0), (1, 0), (0, 0)))[:, :-1]


def wkv7_scan(r, w, k, v, a, b):
    B, S, H, N = r.shape

    def step(state, inp):
        r_t, w_t, k_t, v_t, a_t, b_t = inp
        sa = jnp.einsum('bhvk,bhk->bhv', state, a_t)
        state = (state * w_t[:, :, None, :]
                 + sa[..., None] * b_t[:, :, None, :]
                 + v_t[..., None] * k_t[:, :, None, :])
        y_t = jnp.einsum('bhvk,bhk->bhv', state, r_t)
        return state, y_t

    xs = tuple(jnp.swapaxes(t, 0, 1) for t in (r, w, k, v, a, b))
    s0 = jnp.zeros((B, H, N, N), jnp.float32)
    _, ys = lax.scan(step, s0, xs)
    return jnp.swapaxes(ys, 0, 1)


def rwkv7_mixer(h, mu, w_r, w_k, w_v, w_o, w0, w1, w2, a0, a1, a2,
                g1, g2, k_k, k_a, r_k, ln_w, ln_b):
    B, S, D = h.shape
    H, N = RWKV_HEADS, RWKV_HEAD_DIM
    f32 = jnp.float32
    xx = token_shift(h) - h
    xr = h + xx * mu[0]
    xw = h + xx * mu[1]
    xk = h + xx * mu[2]
    xv = h + xx * mu[3]
    xa = h + xx * mu[4]
    xg = h + xx * mu[5]
    r = xr @ w_r
    k = xk @ w_k
    v = xv @ w_v
    w = -jax.nn.softplus(-(w0 + jnp.tanh(xw @ w1) @ w2)) - 0.5
    a = jax.nn.sigmoid(a0 + (xa @ a1) @ a2)
    g = jax.nn.sigmoid(xg @ g1) @ g2

    def heads(t):
        return t.reshape(B, S, H, N).astype(f32)

    kk = heads(k * k_k)
    kk = kk * lax.rsqrt(jnp.maximum(jnp.sum(kk * kk, -1, keepdims=True), 1e-24))
    k = k * (1 + (a - 1) * k_a)
    r_h, k_h, v_h, a_h = heads(r), heads(k), heads(v), heads(a)
    decay = jnp.exp(-jnp.exp(heads(w)))
    y = wkv7_scan(r_h, decay, k_h, v_h, -kk, kk * a_h)
    mean = jnp.mean(y, -1, keepdims=True)
    var = jnp.mean(jnp.square(y - mean), -1, keepdims=True)
    y = ((y - mean) * lax.rsqrt(var + GN_EPS)).reshape(B, S, D)
    y = y * ln_w.astype(f32) + ln_b.astype(f32)
    bonus = jnp.sum(r_h * k_h * r_k.astype(f32), -1, keepdims=True) * v_h
    y = (y + bonus.reshape(B, S, D)).astype(h.dtype)
    return (y * g) @ w_o


def conv_ffn(h, w_up, conv_w, conv_b, w_down):
    S = h.shape[1]
    z = h @ w_up
    zp = jnp.pad(z, ((0, 0), (CONV_WIDTH - 1, 0), (0, 0)))
    z = sum(zp[:, j:j + S] * conv_w[j] for j in range(CONV_WIDTH)) + conv_b
    gate, val = jnp.split(z, 2, axis=-1)
    return (jax.nn.silu(gate) * val) @ w_down


def setup_inputs(seed: int = 0) -> dict:
    key = jax.random.key(seed)
    ks = iter(jax.random.split(key, 40))
    f32 = jnp.float32

    def nrm(shape, scale):
        return jax.random.normal(next(ks), shape, f32) * scale

    def gain(shape):
        return 1.0 + 0.05 * jax.random.normal(next(ks), shape, f32)

    D, NA, NB = D_MODEL, N_A_LAYERS, N_B_LAYERS
    H2, F2 = 2 * SGU_HALF, 2 * FFN_DIM
    return {
        "x": nrm((BATCH, SEQ, D), 1.0),
        "norm_mix_g": gain((DEPTH, D)),
        "norm_ffn_g": gain((DEPTH, D)),
        "norm_final_g": gain((D,)),
        "sgu_w_in": nrm((NA, D, H2), D ** -0.5),
        "sgu_b_in": nrm((NA, H2), 0.02),
        "sgu_g_v": gain((NA, SGU_HALF)),
        "sgu_w_s": nrm((NA, SGU_GROUPS, SGU_CHUNK, SGU_CHUNK), 0.5 * SGU_CHUNK ** -0.5),
        "sgu_b_s": gain((NA, SGU_GROUPS, SGU_CHUNK)),
        "sgu_w_out": nrm((NA, SGU_HALF, D), SGU_HALF ** -0.5),
        "rwkv_mu": jax.random.uniform(next(ks), (NB, N_SHIFT_MIX, D), f32),
        "rwkv_w_r": nrm((NB, D, D), D ** -0.5),
        "rwkv_w_k": nrm((NB, D, D), D ** -0.5),
        "rwkv_w_v": nrm((NB, D, D), D ** -0.5),
        "rwkv_w_o": nrm((NB, D, D), D ** -0.5),
        "rwkv_w0": jax.random.uniform(next(ks), (NB, D), f32, minval=-6.0, maxval=-1.0),
        "rwkv_w1": nrm((NB, D, DECAY_LORA), 0.1 * D ** -0.5),
        "rwkv_w2": nrm((NB, DECAY_LORA, D), 0.1 * DECAY_LORA ** -0.5),
        "rwkv_a0": nrm((NB, D), 0.1),
        "rwkv_a1": nrm((NB, D, AAA_LORA), 0.1 * D ** -0.5),
        "rwkv_a2": nrm((NB, AAA_LORA, D), 0.1 * AAA_LORA ** -0.5),
        "rwkv_g1": nrm((NB, D, GATE_LORA), D ** -0.5),
        "rwkv_g2": nrm((NB, GATE_LORA, D), GATE_LORA ** -0.5),
        "rwkv_k_k": 0.85 + 0.05 * jax.random.normal(next(ks), (NB, D), f32),
        "rwkv_k_a": gain((NB, D)),
        "rwkv_r_k": nrm((NB, RWKV_HEADS, RWKV_HEAD_DIM), 0.1),
        "rwkv_ln_w": gain((NB, D)),
        "rwkv_ln_b": nrm((NB, D), 0.02),
        "ffn_w_up": nrm((DEPTH, D, F2), D ** -0.5),
        "ffn_conv_w": nrm((DEPTH, CONV_WIDTH, F2), CONV_WIDTH ** -0.5),
        "ffn_conv_b": nrm((DEPTH, F2), 0.02),
        "ffn_w_down": nrm((DEPTH, FFN_DIM, D), FFN_DIM ** -0.5),
    }


def reference(x, norm_mix_g, norm_ffn_g, norm_final_g,
              sgu_w_in, sgu_b_in, sgu_g_v, sgu_w_s, sgu_b_s, sgu_w_out,
              rwkv_mu, rwkv_w_r, rwkv_w_k, rwkv_w_v, rwkv_w_o,
              rwkv_w0, rwkv_w1, rwkv_w2, rwkv_a0, rwkv_a1, rwkv_a2,
              rwkv_g1, rwkv_g2, rwkv_k_k, rwkv_k_a, rwkv_r_k, rwkv_ln_w, rwkv_ln_b,
              ffn_w_up, ffn_conv_w, ffn_conv_b, ffn_w_down):
    h = x
    for i in range(DEPTH):
        hn = rms_norm(h, norm_mix_g[i])
        j = i // N_MIXERS
        if i % N_MIXERS == 0:
            h = h + sgu_mixer(hn, sgu_w_in[j], sgu_b_in[j], sgu_g_v[j],
                              sgu_w_s[j], sgu_b_s[j], sgu_w_out[j])
        else:
            h = h + rwkv7_mixer(hn, rwkv_mu[j], rwkv_w_r[j], rwkv_w_k[j], rwkv_w_v[j], rwkv_w_o[j],
                                rwkv_w0[j], rwkv_w1[j], rwkv_w2[j],
                                rwkv_a0[j], rwkv_a1[j], rwkv_a2[j],
                                rwkv_g1[j], rwkv_g2[j], rwkv_k_k[j], rwkv_k_a[j], rwkv_r_k[j],
                                rwkv_ln_w[j], rwkv_ln_b[j])
        h = h + conv_ffn(rms_norm(h, norm_ffn_g[i]), ffn_w_up[i], ffn_conv_w[i],
                         ffn_conv_b[i], ffn_w_down[i])
    return rms_norm(h, norm_final_g)
```

```python
import functools
import math

import jax
import jax.numpy as jnp
from jax import lax
from jax.experimental import pallas as pl
from jax.experimental.pallas import tpu as pltpu

F32 = jnp.float32
BF16 = jnp.bfloat16

RMS_EPS = 1e-6
GN_EPS = 64e-5
HEAD_DIM = 64
LANES = 128
SGU_BLOCK = 128
SGU_GROUP_DIM = 64
WKV_CHUNK = 64
CONV_WIDTH = 3
VMEM_LIMIT_BYTES = 56 * 1024 * 1024

SGU_TILE = 512
FFN_TILE = 512
PROJ_TILE = 256
OUT_TILE = 512


def _rms(x, g):
    ms = jnp.mean(x * x, axis=-1, keepdims=True)
    return x * lax.rsqrt(ms + RMS_EPS) * g


def _dot(a, b):
    return jnp.dot(a, b, preferred_element_type=F32)


def _dot_nt(a, b):
    return lax.dot_general(a, b, (((1,), (1,)), ((), ())), preferred_element_type=F32)


def _dot_tn(a, b):
    return lax.dot_general(a, b, (((0,), (0,)), ((), ())), preferred_element_type=F32)


def _const_spec(shape):
    nd = len(shape)
    return pl.BlockSpec(shape, lambda *_: (0,) * nd, pipeline_mode=pl.Buffered(1))


def _params(sem):
    return pltpu.CompilerParams(dimension_semantics=sem, vmem_limit_bytes=VMEM_LIMIT_BYTES)


def _sgu_kernel(h_ref, gmix_ref, win_ref, bin_ref, gv_ref, ws_ref, bs_ref, wout_ref,
                o_ref, sv_ref):
    tm, d = h_ref.shape
    h = h_ref[...]
    hn = _rms(h, gmix_ref[...]).astype(BF16)

    def gelu(z):
        return 0.5 * z * (1.0 + lax.erf(z * (1.0 / math.sqrt(2.0))))

    v = gelu(_dot(hn, win_ref[:, d:]) + bin_ref[:, d:])
    v = _rms(v, gv_ref[...]).astype(BF16)

    lane = lax.broadcasted_iota(jnp.int32, (SGU_BLOCK, LANES), 1)
    lo = lane < SGU_GROUP_DIM
    wrow = lax.broadcasted_iota(jnp.int32, (SGU_BLOCK, 2 * SGU_BLOCK), 0)
    wcol = lax.broadcasted_iota(jnp.int32, (SGU_BLOCK, 2 * SGU_BLOCK), 1)
    causal = (wcol & (SGU_BLOCK - 1)) <= wrow
    zero = jnp.zeros((SGU_BLOCK, LANES), BF16)
    for j in range(d // LANES):
        wj = jnp.where(causal, ws_ref[j], jnp.zeros_like(ws_ref[j]))
        cols = slice(j * LANES, (j + 1) * LANES)
        for n in range(tm // SGU_BLOCK):
            rows = slice(n * SGU_BLOCK, (n + 1) * SGU_BLOCK)
            vb = v[rows, cols]
            rhs = jnp.concatenate([jnp.where(lo, vb, zero), jnp.where(lo, zero, vb)], axis=0)
            sv_ref[rows, cols] = _dot(wj, rhs) + bs_ref[:, cols]

    u = gelu(_dot(hn, win_ref[:, :d]) + bin_ref[:, :d])
    y = (u * sv_ref[...]).astype(BF16)
    o_ref[...] = h + _dot(y, wout_ref[...])


def _sgu_layer(h2, gmix, w_in, b_in, g_v, w_s, b_s, w_out):
    t, d = h2.shape
    tm = SGU_TILE
    groups = w_s.shape[0]
    ws2 = (w_s.reshape(groups // 2, 2, SGU_BLOCK, SGU_BLOCK)
           .transpose(0, 2, 1, 3).reshape(groups // 2, SGU_BLOCK, 2 * SGU_BLOCK).astype(BF16))
    bs_exp = jnp.repeat(b_s.T, SGU_GROUP_DIM, axis=1)
    return pl.pallas_call(
        _sgu_kernel,
        grid=(t // tm,),
        in_specs=[
            pl.BlockSpec((tm, d), lambda i: (i, 0)),
            _const_spec((1, d)),
            _const_spec((d, 2 * d)),
            _const_spec((1, 2 * d)),
            _const_spec((1, d)),
            _const_spec(ws2.shape),
            _const_spec(bs_exp.shape),
            _const_spec((d, d)),
        ],
        out_specs=pl.BlockSpec((tm, d), lambda i: (i, 0)),
        out_shape=jax.ShapeDtypeStruct((t, d), F32),
        scratch_shapes=[pltpu.VMEM((tm, d), F32)],
        compiler_params=_params(("parallel",)),
        name="sgu_mixer",
    )(h2, gmix.reshape(1, d), w_in.astype(BF16), b_in.reshape(1, 2 * d), g_v.reshape(1, d),
      ws2, bs_exp, w_out.astype(BF16))


def _ffn_kernel(h_ref, g_ref, wup_ref, cw_ref, cb_ref, wdn_ref, gfin_ref, o_ref, carry_ref,
                *, n_chunks, final_norm):
    tm, d = h_ref.shape[1], h_ref.shape[2]
    f = wdn_ref.shape[0]
    fc = f // n_chunks

    @pl.when(pl.program_id(1) == 0)
    def _():
        carry_ref[...] = jnp.zeros_like(carry_ref)

    h = h_ref[0]
    hn = _rms(h, g_ref[...]).astype(BF16)
    row = lax.broadcasted_iota(jnp.int32, (tm, 1), 0)

    def conv(cols):
        z = _dot(hn, wup_ref[:, cols])
        c2 = carry_ref[0:1, cols]
        c1 = carry_ref[1:2, cols]
        z1 = jnp.where(row == 0, c1, pltpu.roll(z, 1, 0))
        z2 = jnp.where(row == 0, c2, jnp.where(row == 1, c1, pltpu.roll(z, 2, 0)))
        carry_ref[0:2, cols] = z[tm - 2:tm, :]
        return z2 * cw_ref[0:1, cols] + z1 * cw_ref[1:2, cols] + z * cw_ref[2:3, cols] + cb_ref[:, cols]

    acc = h
    for c in range(n_chunks):
        gate = conv(slice(c * fc, (c + 1) * fc))
        val = conv(slice(f + c * fc, f + (c + 1) * fc))
        act = (gate * jax.nn.sigmoid(gate) * val).astype(BF16)
        acc = acc + _dot(act, wdn_ref[c * fc:(c + 1) * fc, :])
    if final_norm:
        acc = _rms(acc, gfin_ref[...])
    o_ref[0] = acc


def _ffn_layer(h, g, w_up, conv_w, conv_b, w_down, g_final, final_norm):
    b, s, d = h.shape
    f = w_down.shape[0]
    tm = FFN_TILE
    n_chunks = 2
    assert f % (n_chunks * LANES) == 0
    kern = functools.partial(_ffn_kernel, n_chunks=n_chunks, final_norm=final_norm)
    return pl.pallas_call(
        kern,
        grid=(b, s // tm),
        in_specs=[
            pl.BlockSpec((1, tm, d), lambda i, j: (i, j, 0)),
            _const_spec((1, d)),
            _const_spec((d, 2 * f)),
            _const_spec((CONV_WIDTH, 2 * f)),
            _const_spec((1, 2 * f)),
            _const_spec((f, d)),
            _const_spec((1, d)),
        ],
        out_specs=pl.BlockSpec((1, tm, d), lambda i, j: (i, j, 0)),
        out_shape=jax.ShapeDtypeStruct((b, s, d), F32),
        scratch_shapes=[pltpu.VMEM((8, 2 * f), F32)],
        compiler_params=_params(("arbitrary", "arbitrary")),
        name="conv_ffn",
    )(h, g.reshape(1, d), w_up.astype(BF16), conv_w, conv_b.reshape(1, 2 * f),
      w_down.astype(BF16), g_final.reshape(1, d))


def _rwkv_proj_kernel(h_ref, gmix_ref, mu_ref, wr_ref, wk_ref, wv_ref, w0_ref, w1_ref, w2_ref,
                      a0_ref, a1_ref, a2_ref, g1_ref, g2_ref,
                      r_ref, k_ref, v_ref, e_ref, a_ref, g_ref, carry_ref):
    tm = h_ref.shape[1]

    @pl.when(pl.program_id(1) == 0)
    def _():
        carry_ref[...] = jnp.zeros_like(carry_ref)

    hn = _rms(h_ref[0], gmix_ref[...])
    row = lax.broadcasted_iota(jnp.int32, (tm, 1), 0)
    prev = jnp.where(row == 0, carry_ref[0:1, :], pltpu.roll(hn, 1, 0))
    carry_ref[0:1, :] = hn[tm - 1:tm, :]
    xx = prev - hn

    def mix(i):
        return (hn + xx * mu_ref[i:i + 1, :]).astype(BF16)

    r_ref[0] = _dot(mix(0), wr_ref[...])
    wl = w0_ref[...] + _dot(jnp.tanh(_dot(mix(1), w1_ref[...])).astype(BF16), w2_ref[...])
    softplus_neg = jnp.maximum(-wl, 0.0) + jnp.log1p(jnp.exp(-jnp.abs(wl)))
    e_ref[0] = jnp.exp(-softplus_neg - 0.5)
    k_ref[0] = _dot(mix(2), wk_ref[...])
    v_ref[0] = _dot(mix(3), wv_ref[...])
    a_ref[0] = jax.nn.sigmoid(a0_ref[...] + _dot(_dot(mix(4), a1_ref[...]).astype(BF16), a2_ref[...]))
    g_ref[0] = _dot(jax.nn.sigmoid(_dot(mix(5), g1_ref[...])).astype(BF16), g2_ref[...])


def _pad_lora(w_a, w_b):
    rank = w_a.shape[1]
    pad = (-rank) % LANES
    return (jnp.pad(w_a, ((0, 0), (0, pad))).astype(BF16),
            jnp.pad(w_b, ((0, pad), (0, 0))).astype(BF16))


def _rwkv_proj(h, gmix, mu, w_r, w_k, w_v, w0, w1, w2, a0, a1, a2, g1, g2):
    b, s, d = h.shape
    tm = PROJ_TILE
    w1p, w2p = _pad_lora(w1, w2)
    a1p, a2p = _pad_lora(a1, a2)
    g1p, g2p = _pad_lora(g1, g2)
    tok = pl.BlockSpec((1, tm, d), lambda i, j: (i, j, 0))
    out = jax.ShapeDtypeStruct((b, s, d), F32)
    return pl.pallas_call(
        _rwkv_proj_kernel,
        grid=(b, s // tm),
        in_specs=[
            tok,
            _const_spec((1, d)),
            _const_spec(mu.shape),
            _const_spec((d, d)), _const_spec((d, d)), _const_spec((d, d)),
            _const_spec((1, d)), _const_spec(w1p.shape), _const_spec(w2p.shape),
            _const_spec((1, d)), _const_spec(a1p.shape), _const_spec(a2p.shape),
            _const_spec(g1p.shape), _const_spec(g2p.shape),
        ],
        out_specs=[tok] * 6,
        out_shape=[out] * 6,
        scratch_shapes=[pltpu.VMEM((8, d), F32)],
        compiler_params=_params(("arbitrary", "arbitrary")),
        name="rwkv_proj",
    )(h, gmix.reshape(1, d), mu, w_r.astype(BF16), w_k.astype(BF16), w_v.astype(BF16),
      w0.reshape(1, d), w1p, w2p, a0.reshape(1, d), a1p, a2p, g1p, g2p)


def _wkv_kernel(r_ref, k_ref, v_ref, e_ref, a_ref, kk_ref, ka_ref, rk_ref, lnw_ref, lnb_ref,
                o_ref, st_ref):
    c = WKV_CHUNK
    d = r_ref.shape[2]

    @pl.when(pl.program_id(1) == 0)
    def _():
        st_ref[...] = jnp.zeros_like(st_ref)

    row = lax.broadcasted_iota(jnp.int32, (c, LANES), 0)
    lane = lax.broadcasted_iota(jnp.int32, (c, LANES), 1)
    lo = lane < HEAD_DIM
    col = lane & (HEAD_DIM - 1)
    strict = col < row
    incl = col <= row
    strict2 = jnp.concatenate([strict, strict], axis=1)
    incl2 = jnp.concatenate([incl, incl], axis=1)
    eye = (col == row).astype(F32)
    srow = lax.broadcasted_iota(jnp.int32, (LANES, LANES), 0)
    scol = lax.broadcasted_iota(jnp.int32, (LANES, LANES), 1)
    same_head = (srow < HEAD_DIM) == (scol < HEAD_DIM)

    def same_block(size):
        shift = size.bit_length() - 1
        return (col >> shift) == (row >> shift)

    def blockdiag(x):
        xb = x.astype(BF16)
        z = jnp.zeros_like(xb)
        return jnp.concatenate([jnp.where(lo, xb, z), jnp.where(lo, z, xb)], axis=0)

    def pair_mm(x, y):
        return _dot(x.astype(BF16), blockdiag(y))

    def head_sum(x):
        z = jnp.zeros_like(x)
        s_lo = jnp.sum(jnp.where(lo, x, z), axis=-1, keepdims=True)
        s_hi = jnp.sum(jnp.where(lo, z, x), axis=-1, keepdims=True)
        return jnp.where(lo, s_lo, s_hi)

    e = e_ref[0]
    trow = lax.broadcasted_iota(jnp.int32, (c, c), 0)
    tcol = lax.broadcasted_iota(jnp.int32, (c, c), 1)
    tri = (tcol <= trow).astype(BF16)
    e_hi = e.astype(BF16)
    rem = e - e_hi.astype(F32)
    e_mid = rem.astype(BF16)
    e_lo = (rem - e_mid.astype(F32)).astype(BF16)
    lp = -(_dot(tri, e_hi) + _dot(tri, e_mid) + _dot(tri, e_lo))
    lp_end = lp[c - 1:c, :]
    f_r = jnp.exp(lp)
    f_a = jnp.exp(lp + e)
    f_inv = jnp.exp(-lp)
    f_end = jnp.exp(lp_end - lp)
    p_end = jnp.exp(lp_end)

    r = r_ref[0]
    k = k_ref[0]
    v = v_ref[0]
    asig = a_ref[0]
    kk = k * kk_ref[...]
    k2 = k * (1.0 + (asig - 1.0) * ka_ref[...])

    pairs = range(d // LANES)
    sls = [slice(j * LANES, (j + 1) * LANES) for j in pairs]

    def staged(fn, *lists):
        return [fn(*args) for args in zip(*lists)]

    kkn = staged(lambda sl: kk[:, sl] * lax.rsqrt(jnp.maximum(head_sum(kk[:, sl] * kk[:, sl]), 1e-24)), sls)
    b_l = staged(lambda x, sl: x * asig[:, sl], kkn, sls)
    a_t = staged(lambda x, sl: -x * f_a[:, sl], kkn, sls)
    r_t = staged(lambda sl: r[:, sl] * f_r[:, sl], sls)
    b_t = staged(lambda x, sl: x * f_inv[:, sl], b_l, sls)
    k_t = staged(lambda sl: k2[:, sl] * f_inv[:, sl], sls)
    v_l = staged(lambda sl: v[:, sl], sls)

    g = staged(lambda at, rt, bt, kt: _dot_nt(
        jnp.concatenate([at, rt], axis=0).astype(BF16),
        jnp.concatenate([blockdiag(bt), blockdiag(kt)], axis=0)), a_t, r_t, b_t, k_t)
    g_top = staged(lambda x: jnp.where(strict2, x[:c], 0.0), g)
    g_bot = staged(lambda x: jnp.where(incl2, x[c:], 0.0), g)
    a_ab = staged(lambda x: x[:, :LANES], g_top)
    a_ak = staged(lambda x: x[:, LANES:], g_top)

    x1 = staged(lambda x: jnp.where(same_block(8), x, 0.0), a_ab)
    t = staged(lambda x: eye + x, x1)
    x2 = staged(pair_mm, x1, x1)
    t = staged(lambda tt, xx: tt + pair_mm(tt, xx), t, x2)
    x4 = staged(pair_mm, x2, x2)
    t = staged(lambda tt, xx: tt + pair_mm(tt, xx), t, x4)
    for size in (16, 32, 64):
        sel = same_block(size) & jnp.logical_not(same_block(size // 2))
        off_t = staged(lambda x, tt: pair_mm(jnp.where(sel, x, 0.0), tt), a_ab, t)
        t = staged(lambda tt, ot: tt + pair_mm(tt, ot), t, off_t)

    av = staged(pair_mm, a_ak, v_l)
    wu = staged(lambda tt, at, avv: _dot(
        tt.astype(BF16), jnp.concatenate([blockdiag(at), blockdiag(avv)], axis=1)), t, a_t, av)

    s0 = [st_ref[j] for j in pairs]
    say = staged(lambda x, rt, s: _dot_nt(
        jnp.concatenate([x[:, :LANES], rt], axis=0).astype(BF16), s.astype(BF16)), wu, r_t, s0)
    sa = staged(lambda x, y: x[:c] + y[:, LANES:], say, wu)
    y = staged(lambda x, gb, s, vv: x[c:] + _dot(
        gb.astype(BF16), jnp.concatenate([blockdiag(s), blockdiag(vv)], axis=0)), say, g_bot, sa, v_l)
    upd = staged(lambda s, vv, bl, sl: _dot_tn(
        jnp.concatenate([s, vv], axis=0).astype(BF16),
        jnp.concatenate([bl * f_end[:, sl], k2[:, sl] * f_end[:, sl]], axis=0).astype(BF16)),
        sa, v_l, b_l, sls)
    for j in pairs:
        st_ref[j] = jnp.where(same_head, s0[j] * p_end[:, sls[j]] + upd[j], 0.0)

    for j in pairs:
        sl = sls[j]
        mean = head_sum(y[j]) * (1.0 / HEAD_DIM)
        dev = y[j] - mean
        var = head_sum(dev * dev) * (1.0 / HEAD_DIM)
        yn = dev * lax.rsqrt(var + GN_EPS) * lnw_ref[:, sl] + lnb_ref[:, sl]
        bonus = head_sum(r[:, sl] * k2[:, sl] * rk_ref[:, sl]) * v_l[j]
        o_ref[0, :, sl] = yn + bonus


def _wkv(r, k, v, e, a, k_k, k_a, r_k, ln_w, ln_b):
    b, s, d = r.shape
    c = WKV_CHUNK
    tok = pl.BlockSpec((1, c, d), lambda i, j: (i, j, 0))
    vec = _const_spec((1, d))
    return pl.pallas_call(
        _wkv_kernel,
        grid=(b, s // c),
        in_specs=[tok] * 5 + [vec] * 5,
        out_specs=tok,
        out_shape=jax.ShapeDtypeStruct((b, s, d), F32),
        scratch_shapes=[pltpu.VMEM((d // LANES, LANES, LANES), F32)],
        compiler_params=_params(("arbitrary", "arbitrary")),
        name="wkv7_chunk",
    )(r, k, v, e, a, k_k.reshape(1, d), k_a.reshape(1, d), r_k.reshape(1, d),
      ln_w.reshape(1, d), ln_b.reshape(1, d))


def _rwkv_out_kernel(h_ref, y_ref, g_ref, wo_ref, o_ref):
    o_ref[...] = h_ref[...] + _dot((y_ref[...] * g_ref[...]).astype(BF16), wo_ref[...])


def _rwkv_out(h2, y2, g2, w_o):
    t, d = h2.shape
    tm = OUT_TILE
    tok = pl.BlockSpec((tm, d), lambda i: (i, 0))
    return pl.pallas_call(
        _rwkv_out_kernel,
        grid=(t // tm,),
        in_specs=[tok, tok, tok, _const_spec((d, d))],
        out_specs=tok,
        out_shape=jax.ShapeDtypeStruct((t, d), F32),
        compiler_params=_params(("parallel",)),
        name="rwkv_out",
    )(h2, y2, g2, w_o.astype(BF16))


def kernel(x, norm_mix_g, norm_ffn_g, norm_final_g, sgu_w_in, sgu_b_in, sgu_g_v, sgu_w_s, sgu_b_s, sgu_w_out, rwkv_mu, rwkv_w_r, rwkv_w_k, rwkv_w_v, rwkv_w_o, rwkv_w0, rwkv_w1, rwkv_w2, rwkv_a0, rwkv_a1, rwkv_a2, rwkv_g1, rwkv_g2, rwkv_k_k, rwkv_k_a, rwkv_r_k, rwkv_ln_w, rwkv_ln_b, ffn_w_up, ffn_conv_w, ffn_conv_b, ffn_w_down):
    b, s, d = x.shape
    depth = norm_mix_g.shape[0]
    h = x
    for i in range(depth):
        j = i // 2
        if i % 2 == 0:
            h = _sgu_layer(h.reshape(b * s, d), norm_mix_g[i], sgu_w_in[j], sgu_b_in[j], sgu_g_v[j],
                           sgu_w_s[j], sgu_b_s[j], sgu_w_out[j]).reshape(b, s, d)
        else:
            r, k, v, e, a, g = _rwkv_proj(h, norm_mix_g[i], rwkv_mu[j], rwkv_w_r[j], rwkv_w_k[j],
                                          rwkv_w_v[j], rwkv_w0[j], rwkv_w1[j], rwkv_w2[j],
                                          rwkv_a0[j], rwkv_a1[j], rwkv_a2[j], rwkv_g1[j], rwkv_g2[j])
            y = _wkv(r, k, v, e, a, rwkv_k_k[j], rwkv_k_a[j], rwkv_r_k[j], rwkv_ln_w[j], rwkv_ln_b[j])
            h = _rwkv_out(h.reshape(b * s, d), y.reshape(b * s, d), g.reshape(b * s, d),
                          rwkv_w_o[j]).reshape(b, s, d)
        h = _ffn_layer(h, norm_ffn_g[i], ffn_w_up[i], ffn_conv_w[i], ffn_conv_b[i], ffn_w_down[i],
                       norm_final_g, final_norm=(i == depth - 1))
    return h
```

```python
import functools
import math

import jax
import jax.numpy as jnp
from jax import lax
from jax.experimental import pallas as pl
from jax.experimental.pallas import tpu as pltpu

F32 = jnp.float32
BF16 = jnp.bfloat16

RMS_EPS = 1e-6
GN_EPS = 64e-5
HEAD_DIM = 64
LANES = 128
SGU_BLOCK = 128
SGU_GROUP_DIM = 64
WKV_CHUNK = 64
WKV_TILE = 256
CONV_WIDTH = 3
VMEM_LIMIT_BYTES = 56 * 1024 * 1024

SGU_TILE = 512
FFN_TILE = 512
PROJ_TILE = 256
OUT_TILE = 512


def _rms(x, g):
    ms = jnp.mean(x * x, axis=-1, keepdims=True)
    return x * lax.rsqrt(ms + RMS_EPS) * g


def _dot(a, b):
    return jnp.dot(a, b, preferred_element_type=F32)


def _dot_nt(a, b):
    return lax.dot_general(a, b, (((1,), (1,)), ((), ())), preferred_element_type=F32)


def _dot_tn(a, b):
    return lax.dot_general(a, b, (((0,), (0,)), ((), ())), preferred_element_type=F32)


def _const_spec(shape):
    nd = len(shape)
    return pl.BlockSpec(shape, lambda *_: (0,) * nd, pipeline_mode=pl.Buffered(1))


def _params(sem):
    return pltpu.CompilerParams(dimension_semantics=sem, vmem_limit_bytes=VMEM_LIMIT_BYTES)


def _sgu_kernel(h_ref, gmix_ref, win_ref, bin_ref, gv_ref, ws_ref, bs_ref, wout_ref,
                o_ref, sv_ref):
    tm, d = h_ref.shape
    h = h_ref[...]
    hn = _rms(h, gmix_ref[...]).astype(BF16)

    def gelu(z):
        return 0.5 * z * (1.0 + lax.erf(z * (1.0 / math.sqrt(2.0))))

    v = gelu(_dot(hn, win_ref[:, d:]) + bin_ref[:, d:])
    v = _rms(v, gv_ref[...]).astype(BF16)

    lane = lax.broadcasted_iota(jnp.int32, (SGU_BLOCK, LANES), 1)
    lo = lane < SGU_GROUP_DIM
    wrow = lax.broadcasted_iota(jnp.int32, (SGU_BLOCK, 2 * SGU_BLOCK), 0)
    wcol = lax.broadcasted_iota(jnp.int32, (SGU_BLOCK, 2 * SGU_BLOCK), 1)
    causal = (wcol & (SGU_BLOCK - 1)) <= wrow
    zero = jnp.zeros((SGU_BLOCK, LANES), BF16)
    for j in range(d // LANES):
        wj = jnp.where(causal, ws_ref[j], jnp.zeros_like(ws_ref[j]))
        cols = slice(j * LANES, (j + 1) * LANES)
        for n in range(tm // SGU_BLOCK):
            rows = slice(n * SGU_BLOCK, (n + 1) * SGU_BLOCK)
            vb = v[rows, cols]
            rhs = jnp.concatenate([jnp.where(lo, vb, zero), jnp.where(lo, zero, vb)], axis=0)
            sv_ref[rows, cols] = _dot(wj, rhs) + bs_ref[:, cols]

    u = gelu(_dot(hn, win_ref[:, :d]) + bin_ref[:, :d])
    y = (u * sv_ref[...]).astype(BF16)
    o_ref[...] = h + _dot(y, wout_ref[...])


def _sgu_layer(h2, gmix, w_in, b_in, g_v, w_s, b_s, w_out):
    t, d = h2.shape
    tm = SGU_TILE
    groups = w_s.shape[0]
    ws2 = (w_s.reshape(groups // 2, 2, SGU_BLOCK, SGU_BLOCK)
           .transpose(0, 2, 1, 3).reshape(groups // 2, SGU_BLOCK, 2 * SGU_BLOCK).astype(BF16))
    bs_exp = jnp.repeat(b_s.T, SGU_GROUP_DIM, axis=1)
    return pl.pallas_call(
        _sgu_kernel,
        grid=(t // tm,),
        in_specs=[
            pl.BlockSpec((tm, d), lambda i: (i, 0)),
            _const_spec((1, d)),
            _const_spec((d, 2 * d)),
            _const_spec((1, 2 * d)),
            _const_spec((1, d)),
            _const_spec(ws2.shape),
            _const_spec(bs_exp.shape),
            _const_spec((d, d)),
        ],
        out_specs=pl.BlockSpec((tm, d), lambda i: (i, 0)),
        out_shape=jax.ShapeDtypeStruct((t, d), F32),
        scratch_shapes=[pltpu.VMEM((tm, d), F32)],
        compiler_params=_params(("parallel",)),
        name="sgu_mixer",
    )(h2, gmix.reshape(1, d), w_in.astype(BF16), b_in.reshape(1, 2 * d), g_v.reshape(1, d),
      ws2, bs_exp, w_out.astype(BF16))


def _ffn_kernel(h_ref, g_ref, wup_ref, cw_ref, cb_ref, wdn_ref, gfin_ref, o_ref, carry_ref,
                *, n_chunks, final_norm):
    tm, d = h_ref.shape[1], h_ref.shape[2]
    f = wdn_ref.shape[0]
    fc = f // n_chunks

    @pl.when(pl.program_id(1) == 0)
    def _():
        carry_ref[...] = jnp.zeros_like(carry_ref)

    h = h_ref[0]
    hn = _rms(h, g_ref[...]).astype(BF16)
    row = lax.broadcasted_iota(jnp.int32, (tm, 1), 0)

    def conv(cols):
        z = _dot(hn, wup_ref[:, cols])
        c2 = carry_ref[0:1, cols]
        c1 = carry_ref[1:2, cols]
        z1 = jnp.where(row == 0, c1, pltpu.roll(z, 1, 0))
        z2 = jnp.where(row == 0, c2, jnp.where(row == 1, c1, pltpu.roll(z, 2, 0)))
        carry_ref[0:2, cols] = z[tm - 2:tm, :]
        return z2 * cw_ref[0:1, cols] + z1 * cw_ref[1:2, cols] + z * cw_ref[2:3, cols] + cb_ref[:, cols]

    acc = h
    for c in range(n_chunks):
        gate = conv(slice(c * fc, (c + 1) * fc))
        val = conv(slice(f + c * fc, f + (c + 1) * fc))
        act = (gate * jax.nn.sigmoid(gate) * val).astype(BF16)
        acc = acc + _dot(act, wdn_ref[c * fc:(c + 1) * fc, :])
    if final_norm:
        acc = _rms(acc, gfin_ref[...])
    o_ref[0] = acc


def _ffn_layer(h, g, w_up, conv_w, conv_b, w_down, g_final, final_norm):
    b, s, d = h.shape
    f = w_down.shape[0]
    tm = FFN_TILE
    n_chunks = 2
    assert f % (n_chunks * LANES) == 0
    kern = functools.partial(_ffn_kernel, n_chunks=n_chunks, final_norm=final_norm)
    return pl.pallas_call(
        kern,
        grid=(b, s // tm),
        in_specs=[
            pl.BlockSpec((1, tm, d), lambda i, j: (i, j, 0)),
            _const_spec((1, d)),
            _const_spec((d, 2 * f)),
            _const_spec((CONV_WIDTH, 2 * f)),
            _const_spec((1, 2 * f)),
            _const_spec((f, d)),
            _const_spec((1, d)),
        ],
        out_specs=pl.BlockSpec((1, tm, d), lambda i, j: (i, j, 0)),
        out_shape=jax.ShapeDtypeStruct((b, s, d), F32),
        scratch_shapes=[pltpu.VMEM((8, 2 * f), F32)],
        compiler_params=_params(("arbitrary", "arbitrary")),
        name="conv_ffn",
    )(h, g.reshape(1, d), w_up.astype(BF16), conv_w, conv_b.reshape(1, 2 * f),
      w_down.astype(BF16), g_final.reshape(1, d))


def _rwkv_proj_kernel(h_ref, gmix_ref, mu_ref, wr_ref, wk_ref, wv_ref, w0_ref, w1_ref, w2_ref,
                      a0_ref, a1_ref, a2_ref, g1_ref, g2_ref,
                      r_ref, k_ref, v_ref, e_ref, a_ref, g_ref, carry_ref):
    tm = h_ref.shape[1]

    @pl.when(pl.program_id(1) == 0)
    def _():
        carry_ref[...] = jnp.zeros_like(carry_ref)

    hn = _rms(h_ref[0], gmix_ref[...])
    row = lax.broadcasted_iota(jnp.int32, (tm, 1), 0)
    prev = jnp.where(row == 0, carry_ref[0:1, :], pltpu.roll(hn, 1, 0))
    carry_ref[0:1, :] = hn[tm - 1:tm, :]
    xx = prev - hn

    def mix(i):
        return (hn + xx * mu_ref[i:i + 1, :]).astype(BF16)

    r_ref[0] = _dot(mix(0), wr_ref[...])
    wl = w0_ref[...] + _dot(jnp.tanh(_dot(mix(1), w1_ref[...])).astype(BF16), w2_ref[...])
    softplus_neg = jnp.maximum(-wl, 0.0) + jnp.log1p(jnp.exp(-jnp.abs(wl)))
    e_ref[0] = jnp.exp(-softplus_neg - 0.5)
    k_ref[0] = _dot(mix(2), wk_ref[...])
    v_ref[0] = _dot(mix(3), wv_ref[...])
    a_ref[0] = jax.nn.sigmoid(a0_ref[...] + _dot(_dot(mix(4), a1_ref[...]).astype(BF16), a2_ref[...]))
    g_ref[0] = _dot(jax.nn.sigmoid(_dot(mix(5), g1_ref[...])).astype(BF16), g2_ref[...])


def _pad_lora(w_a, w_b):
    rank = w_a.shape[1]
    pad = (-rank) % LANES
    return (jnp.pad(w_a, ((0, 0), (0, pad))).astype(BF16),
            jnp.pad(w_b, ((0, pad), (0, 0))).astype(BF16))


def _rwkv_proj(h, gmix, mu, w_r, w_k, w_v, w0, w1, w2, a0, a1, a2, g1, g2):
    b, s, d = h.shape
    tm = PROJ_TILE
    w1p, w2p = _pad_lora(w1, w2)
    a1p, a2p = _pad_lora(a1, a2)
    g1p, g2p = _pad_lora(g1, g2)
    tok = pl.BlockSpec((1, tm, d), lambda i, j: (i, j, 0))
    out = jax.ShapeDtypeStruct((b, s, d), F32)
    return pl.pallas_call(
        _rwkv_proj_kernel,
        grid=(b, s // tm),
        in_specs=[
            tok,
            _const_spec((1, d)),
            _const_spec(mu.shape),
            _const_spec((d, d)), _const_spec((d, d)), _const_spec((d, d)),
            _const_spec((1, d)), _const_spec(w1p.shape), _const_spec(w2p.shape),
            _const_spec((1, d)), _const_spec(a1p.shape), _const_spec(a2p.shape),
            _const_spec(g1p.shape), _const_spec(g2p.shape),
        ],
        out_specs=[tok] * 6,
        out_shape=[out] * 6,
        scratch_shapes=[pltpu.VMEM((8, d), F32)],
        compiler_params=_params(("arbitrary", "arbitrary")),
        name="rwkv_proj",
    )(h, gmix.reshape(1, d), mu, w_r.astype(BF16), w_k.astype(BF16), w_v.astype(BF16),
      w0.reshape(1, d), w1p, w2p, a0.reshape(1, d), a1p, a2p, g1p, g2p)


def _wkv_kernel(r_ref, k_ref, v_ref, e_ref, a_ref, kk_ref, ka_ref, rk_ref, lnw_ref, lnb_ref,
                o_ref, st_ref):
    c = WKV_CHUNK
    tc, d = r_ref.shape[1], r_ref.shape[2]
    n_chunks = tc // c
    n_pairs = d // LANES

    @pl.when(pl.program_id(1) == 0)
    def _():
        st_ref[...] = jnp.zeros_like(st_ref)

    row = lax.broadcasted_iota(jnp.int32, (c, LANES), 0)
    lane = lax.broadcasted_iota(jnp.int32, (c, LANES), 1)
    lo = lane < HEAD_DIM
    col = lane & (HEAD_DIM - 1)
    strict = col < row
    incl = col <= row
    strict2 = jnp.concatenate([strict, strict], axis=1)
    incl2 = jnp.concatenate([incl, incl], axis=1)
    eye = (col == row).astype(F32)
    srow = lax.broadcasted_iota(jnp.int32, (LANES, LANES), 0)
    scol = lax.broadcasted_iota(jnp.int32, (LANES, LANES), 1)
    same_head = (srow < HEAD_DIM) == (scol < HEAD_DIM)
    trow = lax.broadcasted_iota(jnp.int32, (c, c), 0)
    tcol = lax.broadcasted_iota(jnp.int32, (c, c), 1)
    tri = (tcol <= trow).astype(BF16)

    def same_block(size):
        shift = size.bit_length() - 1
        return (col >> shift) == (row >> shift)

    def blockdiag(x):
        xb = x.astype(BF16)
        z = jnp.zeros_like(xb)
        return jnp.concatenate([jnp.where(lo, xb, z), jnp.where(lo, z, xb)], axis=0)

    def pair_mm(x, y):
        return _dot(x.astype(BF16), blockdiag(y))

    def head_sum(x):
        z = jnp.zeros_like(x)
        s_lo = jnp.sum(jnp.where(lo, x, z), axis=-1, keepdims=True)
        s_hi = jnp.sum(jnp.where(lo, z, x), axis=-1, keepdims=True)
        return jnp.where(lo, s_lo, s_hi)

    def staged(fn, *lists):
        return [fn(*args) for args in zip(*lists)]

    r_t, a_t, b_t, k_t, b_e, k_e, v_l, p_end, rkv = [], [], [], [], [], [], [], [], []
    for q in range(n_chunks):
        rows = slice(q * c, (q + 1) * c)
        e = e_ref[0, rows, :]
        e_hi = e.astype(BF16)
        rem = e - e_hi.astype(F32)
        e_mid = rem.astype(BF16)
        e_lo = (rem - e_mid.astype(F32)).astype(BF16)
        lp = -(_dot(tri, e_hi) + _dot(tri, e_mid) + _dot(tri, e_lo))
        lp_end = lp[c - 1:c, :]
        f_r = jnp.exp(lp)
        f_a = jnp.exp(lp + e)
        f_inv = jnp.exp(-lp)
        f_end = jnp.exp(lp_end - lp)
        pe = jnp.exp(lp_end)
        r = r_ref[0, rows, :]
        k = k_ref[0, rows, :]
        v = v_ref[0, rows, :]
        asig = a_ref[0, rows, :]
        kk = k * kk_ref[...]
        k2 = k * (1.0 + (asig - 1.0) * ka_ref[...])
        rk2 = r * k2 * rk_ref[...]
        for j in range(n_pairs):
            sl = slice(j * LANES, (j + 1) * LANES)
            kkj = kk[:, sl]
            kkn = kkj * lax.rsqrt(jnp.maximum(head_sum(kkj * kkj), 1e-24))
            bj = kkn * asig[:, sl]
            a_t.append(-kkn * f_a[:, sl])
            r_t.append(r[:, sl] * f_r[:, sl])
            b_t.append(bj * f_inv[:, sl])
            k_t.append(k2[:, sl] * f_inv[:, sl])
            b_e.append(bj * f_end[:, sl])
            k_e.append(k2[:, sl] * f_end[:, sl])
            v_l.append(v[:, sl])
            p_end.append(pe[:, sl])
            rkv.append(head_sum(rk2[:, sl]) * v[:, sl])

    g = staged(lambda at, rt, bt, kt: _dot_nt(
        jnp.concatenate([at, rt], axis=0).astype(BF16),
        jnp.concatenate([blockdiag(bt), blockdiag(kt)], axis=0)), a_t, r_t, b_t, k_t)
    g_top = staged(lambda x: jnp.where(strict2, x[:c], 0.0), g)
    g_bot = staged(lambda x: jnp.where(incl2, x[c:], 0.0), g)
    a_ab = staged(lambda x: x[:, :LANES], g_top)
    a_ak = staged(lambda x: x[:, LANES:], g_top)

    x1 = staged(lambda x: jnp.where(same_block(8), x, 0.0), a_ab)
    t = staged(lambda x: eye + x, x1)
    x2 = staged(pair_mm, x1, x1)
    t = staged(lambda tt, xx: tt + pair_mm(tt, xx), t, x2)
    x4 = staged(pair_mm, x2, x2)
    t = staged(lambda tt, xx: tt + pair_mm(tt, xx), t, x4)
    for size in (16, 32, 64):
        sel = same_block(size) & jnp.logical_not(same_block(size // 2))
        off_t = staged(lambda x, tt: pair_mm(jnp.where(sel, x, 0.0), tt), a_ab, t)
        t = staged(lambda tt, ot: tt + pair_mm(tt, ot), t, off_t)

    av = staged(pair_mm, a_ak, v_l)
    wu = staged(lambda tt, at, avv: _dot(
        tt.astype(BF16), jnp.concatenate([blockdiag(at), blockdiag(avv)], axis=1)), t, a_t, av)
    w = staged(lambda x: x[:, :LANES], wu)
    u = staged(lambda x: x[:, LANES:], wu)
    qm = staged(lambda rt, gb, ww: rt + pair_mm(gb[:, :LANES], ww), r_t, g_bot, w)
    y0 = staged(lambda gb, uu, vv: _dot(
        gb.astype(BF16), jnp.concatenate([blockdiag(uu), blockdiag(vv)], axis=0)), g_bot, u, v_l)
    m_st = staged(lambda ww, be: jnp.where(same_head, _dot_tn(ww.astype(BF16), be.astype(BF16)), 0.0),
                  w, b_e)
    n_st = staged(lambda uu, vv, be, ke: jnp.where(same_head, _dot_tn(
        jnp.concatenate([uu, vv], axis=0).astype(BF16),
        jnp.concatenate([be, ke], axis=0).astype(BF16)), 0.0), u, v_l, b_e, k_e)

    state = [st_ref[j] for j in range(n_pairs)]
    for q in range(n_chunks):
        base = q * n_pairs
        s_bf = [s.astype(BF16) for s in state]
        new_state = [state[j] * p_end[base + j] + _dot(s_bf[j], m_st[base + j].astype(BF16))
                     + n_st[base + j] for j in range(n_pairs)]
        y = [y0[base + j] + _dot_nt(qm[base + j].astype(BF16), s_bf[j]) for j in range(n_pairs)]
        state = new_state
        for j in range(n_pairs):
            sl = slice(j * LANES, (j + 1) * LANES)
            mean = head_sum(y[j]) * (1.0 / HEAD_DIM)
            dev = y[j] - mean
            var = head_sum(dev * dev) * (1.0 / HEAD_DIM)
            yn = dev * lax.rsqrt(var + GN_EPS) * lnw_ref[:, sl] + lnb_ref[:, sl]
            o_ref[0, q * c:(q + 1) * c, sl] = yn + rkv[base + j]
    for j in range(n_pairs):
        st_ref[j] = state[j]


def _wkv(r, k, v, e, a, k_k, k_a, r_k, ln_w, ln_b):
    b, s, d = r.shape
    tc = WKV_TILE
    tok = pl.BlockSpec((1, tc, d), lambda i, j: (i, j, 0))
    vec = _const_spec((1, d))
    return pl.pallas_call(
        _wkv_kernel,
        grid=(b, s // tc),
        in_specs=[tok] * 5 + [vec] * 5,
        out_specs=tok,
        out_shape=jax.ShapeDtypeStruct((b, s, d), F32),
        scratch_shapes=[pltpu.VMEM((d // LANES, LANES, LANES), F32)],
        compiler_params=_params(("arbitrary", "arbitrary")),
        name="wkv7_chunk",
    )(r, k, v, e, a, k_k.reshape(1, d), k_a.reshape(1, d), r_k.reshape(1, d),
      ln_w.reshape(1, d), ln_b.reshape(1, d))


def _rwkv_out_kernel(h_ref, y_ref, g_ref, wo_ref, o_ref):
    o_ref[...] = h_ref[...] + _dot((y_ref[...] * g_ref[...]).astype(BF16), wo_ref[...])


def _rwkv_out(h2, y2, g2, w_o):
    t, d = h2.shape
    tm = OUT_TILE
    tok = pl.BlockSpec((tm, d), lambda i: (i, 0))
    return pl.pallas_call(
        _rwkv_out_kernel,
        grid=(t // tm,),
        in_specs=[tok, tok, tok, _const_spec((d, d))],
        out_specs=tok,
        out_shape=jax.ShapeDtypeStruct((t, d), F32),
        compiler_params=_params(("parallel",)),
        name="rwkv_out",
    )(h2, y2, g2, w_o.astype(BF16))


def kernel(x, norm_mix_g, norm_ffn_g, norm_final_g, sgu_w_in, sgu_b_in, sgu_g_v, sgu_w_s, sgu_b_s, sgu_w_out, rwkv_mu, rwkv_w_r, rwkv_w_k, rwkv_w_v, rwkv_w_o, rwkv_w0, rwkv_w1, rwkv_w2, rwkv_a0, rwkv_a1, rwkv_a2, rwkv_g1, rwkv_g2, rwkv_k_k, rwkv_k_a, rwkv_r_k, rwkv_ln_w, rwkv_ln_b, ffn_w_up, ffn_conv_w, ffn_conv_b, ffn_w_down):
    b, s, d = x.shape
    depth = norm_mix_g.shape[0]
    h = x
    for i in range(depth):
        j = i // 2
        if i % 2 == 0:
            h = _sgu_layer(h.reshape(b * s, d), norm_mix_g[i], sgu_w_in[j], sgu_b_in[j], sgu_g_v[j],
                           sgu_w_s[j], sgu_b_s[j], sgu_w_out[j]).reshape(b, s, d)
        else:
            r, k, v, e, a, g = _rwkv_proj(h, norm_mix_g[i], rwkv_mu[j], rwkv_w_r[j], rwkv_w_k[j],
                                          rwkv_w_v[j], rwkv_w0[j], rwkv_w1[j], rwkv_w2[j],
                                          rwkv_a0[j], rwkv_a1[j], rwkv_a2[j], rwkv_g1[j], rwkv_g2[j])
            y = _wkv(r, k, v, e, a, rwkv_k_k[j], rwkv_k_a[j], rwkv_r_k[j], rwkv_ln_w[j], rwkv_ln_b[j])
            h = _rwkv_out(h.reshape(b * s, d), y.reshape(b * s, d), g.reshape(b * s, d),
                          rwkv_w_o[j]).reshape(b, s, d)
        h = _ffn_layer(h, norm_ffn_g[i], ffn_w_up[i], ffn_conv_w[i], ffn_conv_b[i], ffn_w_down[i],
                       norm_final_g, final_norm=(i == depth - 1))
    return h
```

```python
import functools
import math

import jax
import jax.numpy as jnp
from jax import lax
from jax.experimental import pallas as pl
from jax.experimental.pallas import tpu as pltpu

F32 = jnp.float32
BF16 = jnp.bfloat16

RMS_EPS = 1e-6
GN_EPS = 64e-5
HEAD_DIM = 64
LANES = 128
MXU_TILE = 256
SGU_BLOCK = 128
SGU_GROUP_DIM = 64
WKV_CHUNK = 64
WKV_TILE = 256
CONV_WIDTH = 3
VMEM_LIMIT_BYTES = 56 * 1024 * 1024

SGU_TILE = 512
FFN_TILE = 512
PROJ_TILE = 512


def _rms(x, g):
    ms = jnp.mean(x * x, axis=-1, keepdims=True)
    return x * lax.rsqrt(ms + RMS_EPS) * g


def _dot(a, b):
    return jnp.dot(a, b, preferred_element_type=F32)


def _dot_nt(a, b):
    return lax.dot_general(a, b, (((1,), (1,)), ((), ())), preferred_element_type=F32)


def _dot_tn(a, b):
    return lax.dot_general(a, b, (((0,), (0,)), ((), ())), preferred_element_type=F32)


def _const_spec(shape):
    nd = len(shape)
    return pl.BlockSpec(shape, lambda *_: (0,) * nd, pipeline_mode=pl.Buffered(1))


def _params(sem):
    return pltpu.CompilerParams(dimension_semantics=sem, vmem_limit_bytes=VMEM_LIMIT_BYTES)


def _sgu_kernel(h_ref, gmix_ref, win_ref, bin_ref, gv_ref, ws_ref, bs_ref, wout_ref,
                o_ref, sv_ref):
    tm, d = h_ref.shape
    h = h_ref[...]
    hn = _rms(h, gmix_ref[...]).astype(BF16)

    def gelu(z):
        return 0.5 * z * (1.0 + lax.erf(z * (1.0 / math.sqrt(2.0))))

    v = gelu(_dot(hn, win_ref[:, d:]) + bin_ref[:, d:])
    v = _rms(v, gv_ref[...]).astype(BF16)

    lane = lax.broadcasted_iota(jnp.int32, (SGU_BLOCK, LANES), 1)
    lo = lane < SGU_GROUP_DIM
    wrow = lax.broadcasted_iota(jnp.int32, (SGU_BLOCK, 2 * SGU_BLOCK), 0)
    wcol = lax.broadcasted_iota(jnp.int32, (SGU_BLOCK, 2 * SGU_BLOCK), 1)
    causal = (wcol & (SGU_BLOCK - 1)) <= wrow
    zero = jnp.zeros((SGU_BLOCK, LANES), BF16)
    for j in range(d // LANES):
        wj = jnp.where(causal, ws_ref[j], jnp.zeros_like(ws_ref[j]))
        cols = slice(j * LANES, (j + 1) * LANES)
        for n in range(0, tm // SGU_BLOCK, 2):
            rhs = []
            for m in (n, n + 1):
                vb = v[m * SGU_BLOCK:(m + 1) * SGU_BLOCK, cols]
                rhs.append(jnp.concatenate([jnp.where(lo, vb, zero), jnp.where(lo, zero, vb)], axis=0))
            out = _dot(wj, jnp.concatenate(rhs, axis=1))
            for i, m in enumerate((n, n + 1)):
                sv_ref[m * SGU_BLOCK:(m + 1) * SGU_BLOCK, cols] = (
                    out[:, i * LANES:(i + 1) * LANES] + bs_ref[:, cols])

    u = gelu(_dot(hn, win_ref[:, :d]) + bin_ref[:, :d])
    y = (u * sv_ref[...]).astype(BF16)
    o_ref[...] = h + _dot(y, wout_ref[...])


def _sgu_layer(h2, gmix, w_in, b_in, g_v, w_s, b_s, w_out):
    t, d = h2.shape
    tm = SGU_TILE
    groups = w_s.shape[0]
    ws2 = (w_s.reshape(groups // 2, 2, SGU_BLOCK, SGU_BLOCK)
           .transpose(0, 2, 1, 3).reshape(groups // 2, SGU_BLOCK, 2 * SGU_BLOCK).astype(BF16))
    bs_exp = jnp.repeat(b_s.T, SGU_GROUP_DIM, axis=1)
    return pl.pallas_call(
        _sgu_kernel,
        grid=(t // tm,),
        in_specs=[
            pl.BlockSpec((tm, d), lambda i: (i, 0)),
            _const_spec((1, d)),
            _const_spec((d, 2 * d)),
            _const_spec((1, 2 * d)),
            _const_spec((1, d)),
            _const_spec(ws2.shape),
            _const_spec(bs_exp.shape),
            _const_spec((d, d)),
        ],
        out_specs=pl.BlockSpec((tm, d), lambda i: (i, 0)),
        out_shape=jax.ShapeDtypeStruct((t, d), F32),
        scratch_shapes=[pltpu.VMEM((tm, d), F32)],
        compiler_params=_params(("parallel",)),
        name="sgu_mixer",
    )(h2, gmix.reshape(1, d), w_in.astype(BF16), b_in.reshape(1, 2 * d), g_v.reshape(1, d),
      ws2, bs_exp, w_out.astype(BF16))


def _ffn_kernel(h_ref, g_ref, wup_ref, cw_ref, cb_ref, wdn_ref, gfin_ref, o_ref, carry_ref,
                *, bounds, final_norm):
    tm, d = h_ref.shape[1], h_ref.shape[2]
    f = wdn_ref.shape[0]

    @pl.when(pl.program_id(1) == 0)
    def _():
        carry_ref[...] = jnp.zeros_like(carry_ref)

    h = h_ref[0]
    hn = _rms(h, g_ref[...]).astype(BF16)
    row = lax.broadcasted_iota(jnp.int32, (tm, 1), 0)

    def conv(cols):
        z = _dot(hn, wup_ref[:, cols])
        c2 = carry_ref[0:1, cols]
        c1 = carry_ref[1:2, cols]
        z1 = jnp.where(row == 0, c1, pltpu.roll(z, 1, 0))
        z2 = jnp.where(row == 0, c2, jnp.where(row == 1, c1, pltpu.roll(z, 2, 0)))
        carry_ref[0:2, cols] = z[tm - 2:tm, :]
        return z2 * cw_ref[0:1, cols] + z1 * cw_ref[1:2, cols] + z * cw_ref[2:3, cols] + cb_ref[:, cols]

    acc = h
    for lo, hi in zip(bounds[:-1], bounds[1:]):
        gate = conv(slice(lo, hi))
        val = conv(slice(f + lo, f + hi))
        act = (gate * jax.nn.sigmoid(gate) * val).astype(BF16)
        acc = acc + _dot(act, wdn_ref[lo:hi, :])
    if final_norm:
        acc = _rms(acc, gfin_ref[...])
    o_ref[0] = acc


def _ffn_layer(h, g, w_up, conv_w, conv_b, w_down, g_final, final_norm):
    b, s, d = h.shape
    f = w_down.shape[0]
    tm = FFN_TILE
    mid = (f // (2 * MXU_TILE) + (f // MXU_TILE) % 2) * MXU_TILE
    kern = functools.partial(_ffn_kernel, bounds=(0, mid, f), final_norm=final_norm)
    return pl.pallas_call(
        kern,
        grid=(b, s // tm),
        in_specs=[
            pl.BlockSpec((1, tm, d), lambda i, j: (i, j, 0)),
            _const_spec((1, d)),
            _const_spec((d, 2 * f)),
            _const_spec((CONV_WIDTH, 2 * f)),
            _const_spec((1, 2 * f)),
            _const_spec((f, d)),
            _const_spec((1, d)),
        ],
        out_specs=pl.BlockSpec((1, tm, d), lambda i, j: (i, j, 0)),
        out_shape=jax.ShapeDtypeStruct((b, s, d), F32),
        scratch_shapes=[pltpu.VMEM((8, 2 * f), F32)],
        compiler_params=_params(("arbitrary", "arbitrary")),
        name="conv_ffn",
    )(h, g.reshape(1, d), w_up.astype(BF16), conv_w, conv_b.reshape(1, 2 * f),
      w_down.astype(BF16), g_final.reshape(1, d))


def _rwkv_proj_kernel(h_ref, gmix_ref, mu_ref, wr_ref, wk_ref, wv_ref, w0_ref, w1_ref, w2_ref,
                      a0_ref, a1_ref, a2_ref, g1_ref, g2_ref,
                      r_ref, k_ref, v_ref, e_ref, a_ref, g_ref, carry_ref):
    tm = h_ref.shape[1]

    @pl.when(pl.program_id(1) == 0)
    def _():
        carry_ref[...] = jnp.zeros_like(carry_ref)

    hn = _rms(h_ref[0], gmix_ref[...])
    row = lax.broadcasted_iota(jnp.int32, (tm, 1), 0)
    prev = jnp.where(row == 0, carry_ref[0:1, :], pltpu.roll(hn, 1, 0))
    carry_ref[0:1, :] = hn[tm - 1:tm, :]
    xx = prev - hn

    def mix(i):
        return (hn + xx * mu_ref[i:i + 1, :]).astype(BF16)

    w_mid = _dot(mix(1), w1_ref[...])
    a_mid = _dot(mix(4), a1_ref[...])
    g_mid = _dot(mix(5), g1_ref[...])
    r_ref[0] = _dot(mix(0), wr_ref[...])
    wl = w0_ref[...] + _dot(jnp.tanh(w_mid).astype(BF16), w2_ref[...])
    al = a0_ref[...] + _dot(a_mid.astype(BF16), a2_ref[...])
    g_ref[0] = _dot(jax.nn.sigmoid(g_mid).astype(BF16), g2_ref[...])
    k_ref[0] = _dot(mix(2), wk_ref[...])
    softplus_neg = jnp.maximum(-wl, 0.0) + jnp.log1p(jnp.exp(-jnp.abs(wl)))
    e_ref[0] = jnp.exp(-softplus_neg - 0.5)
    v_ref[0] = _dot(mix(3), wv_ref[...])
    a_ref[0] = jax.nn.sigmoid(al)


def _pad_lora(w_a, w_b):
    rank = w_a.shape[1]
    pad = (-rank) % LANES
    return (jnp.pad(w_a, ((0, 0), (0, pad))).astype(BF16),
            jnp.pad(w_b, ((0, pad), (0, 0))).astype(BF16))


def _rwkv_proj(h, gmix, mu, w_r, w_k, w_v, w0, w1, w2, a0, a1, a2, g1, g2):
    b, s, d = h.shape
    tm = PROJ_TILE
    w1p, w2p = _pad_lora(w1, w2)
    a1p, a2p = _pad_lora(a1, a2)
    g1p, g2p = _pad_lora(g1, g2)
    tok = pl.BlockSpec((1, tm, d), lambda i, j: (i, j, 0))
    out = jax.ShapeDtypeStruct((b, s, d), F32)
    return pl.pallas_call(
        _rwkv_proj_kernel,
        grid=(b, s // tm),
        in_specs=[
            tok,
            _const_spec((1, d)),
            _const_spec(mu.shape),
            _const_spec((d, d)), _const_spec((d, d)), _const_spec((d, d)),
            _const_spec((1, d)), _const_spec(w1p.shape), _const_spec(w2p.shape),
            _const_spec((1, d)), _const_spec(a1p.shape), _const_spec(a2p.shape),
            _const_spec(g1p.shape), _const_spec(g2p.shape),
        ],
        out_specs=[tok] * 6,
        out_shape=[out] * 6,
        scratch_shapes=[pltpu.VMEM((8, d), F32)],
        compiler_params=_params(("arbitrary", "arbitrary")),
        name="rwkv_proj",
    )(h, gmix.reshape(1, d), mu, w_r.astype(BF16), w_k.astype(BF16), w_v.astype(BF16),
      w0.reshape(1, d), w1p, w2p, a0.reshape(1, d), a1p, a2p, g1p, g2p)


def _wkv_kernel(r_ref, k_ref, v_ref, e_ref, a_ref, g_ref, h_ref, kk_ref, ka_ref, rk_ref, lnw_ref, lnb_ref,
                wo_ref, o_ref, st_ref, yg_ref):
    c = WKV_CHUNK
    tc, d = r_ref.shape[1], r_ref.shape[2]
    n_chunks = tc // c
    n_pairs = d // LANES

    @pl.when(pl.program_id(1) == 0)
    def _():
        st_ref[...] = jnp.zeros_like(st_ref)

    row = lax.broadcasted_iota(jnp.int32, (c, LANES), 0)
    lane = lax.broadcasted_iota(jnp.int32, (c, LANES), 1)
    lo = lane < HEAD_DIM
    col = lane & (HEAD_DIM - 1)
    strict = col < row
    incl = col <= row
    strict2 = jnp.concatenate([strict, strict], axis=1)
    incl2 = jnp.concatenate([incl, incl], axis=1)
    eye = (col == row).astype(F32)
    srow = lax.broadcasted_iota(jnp.int32, (LANES, LANES), 0)
    scol = lax.broadcasted_iota(jnp.int32, (LANES, LANES), 1)
    same_head = (srow < HEAD_DIM) == (scol < HEAD_DIM)
    trow = lax.broadcasted_iota(jnp.int32, (c, c), 0)
    tcol = lax.broadcasted_iota(jnp.int32, (c, c), 1)
    tri = (tcol <= trow).astype(BF16)

    def same_block(size):
        shift = size.bit_length() - 1
        return (col >> shift) == (row >> shift)

    def blockdiag(x):
        xb = x.astype(BF16)
        z = jnp.zeros_like(xb)
        return jnp.concatenate([jnp.where(lo, xb, z), jnp.where(lo, z, xb)], axis=0)

    def pair_mm(x, y):
        return _dot(x.astype(BF16), blockdiag(y))

    def head_sum(x):
        z = jnp.zeros_like(x)
        s_lo = jnp.sum(jnp.where(lo, x, z), axis=-1, keepdims=True)
        s_hi = jnp.sum(jnp.where(lo, z, x), axis=-1, keepdims=True)
        return jnp.where(lo, s_lo, s_hi)

    def staged(fn, *lists):
        return [fn(*args) for args in zip(*lists)]

    r_t, a_t, b_t, k_t, b_e, k_e, v_l, p_end, rkv = [], [], [], [], [], [], [], [], []
    for q in range(n_chunks):
        rows = slice(q * c, (q + 1) * c)
        e = e_ref[0, rows, :]
        e_hi = e.astype(BF16)
        rem = e - e_hi.astype(F32)
        e_mid = rem.astype(BF16)
        e_lo = (rem - e_mid.astype(F32)).astype(BF16)
        lp = -(_dot(tri, e_hi) + _dot(tri, e_mid) + _dot(tri, e_lo))
        lp_end = lp[c - 1:c, :]
        f_r = jnp.exp(lp)
        f_a = jnp.exp(lp + e)
        f_inv = jnp.exp(-lp)
        f_end = jnp.exp(lp_end - lp)
        pe = jnp.exp(lp_end)
        r = r_ref[0, rows, :]
        k = k_ref[0, rows, :]
        v = v_ref[0, rows, :]
        asig = a_ref[0, rows, :]
        kk = k * kk_ref[...]
        k2 = k * (1.0 + (asig - 1.0) * ka_ref[...])
        rk2 = r * k2 * rk_ref[...]
        for j in range(n_pairs):
            sl = slice(j * LANES, (j + 1) * LANES)
            kkj = kk[:, sl]
            kkn = kkj * lax.rsqrt(jnp.maximum(head_sum(kkj * kkj), 1e-24))
            bj = kkn * asig[:, sl]
            a_t.append(-kkn * f_a[:, sl])
            r_t.append(r[:, sl] * f_r[:, sl])
            b_t.append(bj * f_inv[:, sl])
            k_t.append(k2[:, sl] * f_inv[:, sl])
            b_e.append(bj * f_end[:, sl])
            k_e.append(k2[:, sl] * f_end[:, sl])
            v_l.append(v[:, sl])
            p_end.append(pe[:, sl])
            rkv.append(head_sum(rk2[:, sl]) * v[:, sl])

    g = staged(lambda at, rt, bt, kt: _dot_nt(
        jnp.concatenate([at, rt], axis=0).astype(BF16),
        jnp.concatenate([blockdiag(bt), blockdiag(kt)], axis=0)), a_t, r_t, b_t, k_t)
    g_top = staged(lambda x: jnp.where(strict2, x[:c], 0.0), g)
    g_bot = staged(lambda x: jnp.where(incl2, x[c:], 0.0), g)
    a_ab = staged(lambda x: x[:, :LANES], g_top)
    a_ak = staged(lambda x: x[:, LANES:], g_top)

    x1 = staged(lambda x: jnp.where(same_block(8), x, 0.0), a_ab)
    t = staged(lambda x: eye + x, x1)
    x2 = staged(pair_mm, x1, x1)
    t = staged(lambda tt, xx: tt + pair_mm(tt, xx), t, x2)
    x4 = staged(pair_mm, x2, x2)
    t = staged(lambda tt, xx: tt + pair_mm(tt, xx), t, x4)
    for size in (16, 32, 64):
        sel = same_block(size) & jnp.logical_not(same_block(size // 2))
        off_t = staged(lambda x, tt: pair_mm(jnp.where(sel, x, 0.0), tt), a_ab, t)
        t = staged(lambda tt, ot: tt + pair_mm(tt, ot), t, off_t)

    av = staged(pair_mm, a_ak, v_l)
    wu = staged(lambda tt, at, avv: _dot(
        tt.astype(BF16), jnp.concatenate([blockdiag(at), blockdiag(avv)], axis=1)), t, a_t, av)
    w = staged(lambda x: x[:, :LANES], wu)
    u = staged(lambda x: x[:, LANES:], wu)
    qm = staged(lambda rt, gb, ww: rt + pair_mm(gb[:, :LANES], ww), r_t, g_bot, w)
    y0 = staged(lambda gb, uu, vv: _dot(
        gb.astype(BF16), jnp.concatenate([blockdiag(uu), blockdiag(vv)], axis=0)), g_bot, u, v_l)
    m_st = staged(lambda ww, be: jnp.where(same_head, _dot_tn(ww.astype(BF16), be.astype(BF16)), 0.0),
                  w, b_e)
    n_st = staged(lambda uu, vv, be, ke: jnp.where(same_head, _dot_tn(
        jnp.concatenate([uu, vv], axis=0).astype(BF16),
        jnp.concatenate([be, ke], axis=0).astype(BF16)), 0.0), u, v_l, b_e, k_e)

    state = [st_ref[j] for j in range(n_pairs)]
    for q in range(n_chunks):
        base = q * n_pairs
        s_bf = [s.astype(BF16) for s in state]
        new_state = [state[j] * p_end[base + j] + _dot(s_bf[j], m_st[base + j].astype(BF16))
                     + n_st[base + j] for j in range(n_pairs)]
        y = [y0[base + j] + _dot_nt(qm[base + j].astype(BF16), s_bf[j]) for j in range(n_pairs)]
        state = new_state
        for j in range(n_pairs):
            sl = slice(j * LANES, (j + 1) * LANES)
            mean = head_sum(y[j]) * (1.0 / HEAD_DIM)
            dev = y[j] - mean
            var = head_sum(dev * dev) * (1.0 / HEAD_DIM)
            yn = dev * lax.rsqrt(var + GN_EPS) * lnw_ref[:, sl] + lnb_ref[:, sl]
            rows = slice(q * c, (q + 1) * c)
            yg_ref[rows, sl] = ((yn + rkv[base + j]) * g_ref[0, rows, sl]).astype(BF16)
    for j in range(n_pairs):
        st_ref[j] = state[j]
    o_ref[0] = h_ref[0] + _dot(yg_ref[...], wo_ref[...])


def _wkv(r, k, v, e, a, g, h, k_k, k_a, r_k, ln_w, ln_b, w_o):
    b, s, d = r.shape
    tc = WKV_TILE
    tok = pl.BlockSpec((1, tc, d), lambda i, j: (i, j, 0))
    vec = _const_spec((1, d))
    return pl.pallas_call(
        _wkv_kernel,
        grid=(b, s // tc),
        in_specs=[tok] * 7 + [vec] * 5 + [_const_spec((d, d))],
        out_specs=tok,
        out_shape=jax.ShapeDtypeStruct((b, s, d), F32),
        scratch_shapes=[pltpu.VMEM((d // LANES, LANES, LANES), F32), pltpu.VMEM((tc, d), BF16)],
        compiler_params=_params(("arbitrary", "arbitrary")),
        name="wkv7_chunk",
    )(r, k, v, e, a, g, h, k_k.reshape(1, d), k_a.reshape(1, d), r_k.reshape(1, d),
      ln_w.reshape(1, d), ln_b.reshape(1, d), w_o.astype(BF16))


def kernel(x, norm_mix_g, norm_ffn_g, norm_final_g, sgu_w_in, sgu_b_in, sgu_g_v, sgu_w_s, sgu_b_s, sgu_w_out, rwkv_mu, rwkv_w_r, rwkv_w_k, rwkv_w_v, rwkv_w_o, rwkv_w0, rwkv_w1, rwkv_w2, rwkv_a0, rwkv_a1, rwkv_a2, rwkv_g1, rwkv_g2, rwkv_k_k, rwkv_k_a, rwkv_r_k, rwkv_ln_w, rwkv_ln_b, ffn_w_up, ffn_conv_w, ffn_conv_b, ffn_w_down):
    b, s, d = x.shape
    depth = norm_mix_g.shape[0]
    h = x
    for i in range(depth):
        j = i // 2
        if i % 2 == 0:
            h = _sgu_layer(h.reshape(b * s, d), norm_mix_g[i], sgu_w_in[j], sgu_b_in[j], sgu_g_v[j],
                           sgu_w_s[j], sgu_b_s[j], sgu_w_out[j]).reshape(b, s, d)
        else:
            r, k, v, e, a, g = _rwkv_proj(h, norm_mix_g[i], rwkv_mu[j], rwkv_w_r[j], rwkv_w_k[j],
                                          rwkv_w_v[j], rwkv_w0[j], rwkv_w1[j], rwkv_w2[j],
                                          rwkv_a0[j], rwkv_a1[j], rwkv_a2[j], rwkv_g1[j], rwkv_g2[j])
            h = _wkv(r, k, v, e, a, g, h, rwkv_k_k[j], rwkv_k_a[j], rwkv_r_k[j], rwkv_ln_w[j],
                     rwkv_ln_b[j], rwkv_w_o[j])
        h = _ffn_layer(h, norm_ffn_g[i], ffn_w_up[i], ffn_conv_w[i], ffn_conv_b[i], ffn_w_down[i],
                       norm_final_g, final_norm=(i == depth - 1))
    return h
```
